```python
import jax
import jax.numpy as jnp
from jax import lax
import numpy as np

D_MODEL = 1024
BATCH = 8
SEQ = 2048
DEPTH = 4
DEC_BATCH = 128
DEC_SEQ = 1
PAST_LEN = 2048
PAGE_SIZE = 128

HEAD_DIM = 64
ATT_WIDTH = D_MODEL // 2
RWKV_WIDTH = D_MODEL - ATT_WIDTH
H_ATT = ATT_WIDTH // HEAD_DIM
H_RWKV = RWKV_WIDTH // HEAD_DIM
DECAY_LORA = 64
ICLR_LORA = 64
GATE_LORA = 128
D_FF = 2816
Q_BLOCK = 128
FORGET_BIAS = 7.0
P_ATT = 3 * ATT_WIDTH + H_ATT
P_RWKV = 3 * RWKV_WIDTH + DECAY_LORA + ICLR_LORA + GATE_LORA
P_IN = P_ATT + P_RWKV
N_MOD = 9
DN_ALPHA = (2 * DEPTH) ** 0.25
DN_BETA = (8 * DEPTH) ** -0.25
LN_EPS = 1e-5
GN_EPS = 64e-5

kernel_name = 'fox_rwkv7_macaron_deepnorm_step'


def layer_norm(x, g, b):
    xf = x.astype(jnp.float32)
    mu = jnp.mean(xf, axis=-1, keepdims=True)
    var = jnp.mean(jnp.square(xf - mu), axis=-1, keepdims=True)
    return ((xf - mu) * lax.rsqrt(var + LN_EPS) * g + b).astype(x.dtype)


def swiglu(h, w_up, w_down):
    gate, up = jnp.split(h @ w_up, 2, axis=-1)
    return (jax.nn.silu(gate) * up) @ w_down


def fox_prompt(q, k, v, logf):
    b, s, h, dh = q.shape
    nb = s // Q_BLOCK
    c = jnp.cumsum(logf, axis=1).transpose(0, 2, 1)
    q_blocks = jnp.moveaxis(q.reshape(b, nb, Q_BLOCK, h, dh), 1, 0)
    c_blocks = jnp.moveaxis(c.reshape(b, h, nb, Q_BLOCK), 2, 0)
    k_pos = jnp.arange(s)

    def block(args):
        q_i, c_i, i = args
        logits = jnp.einsum('bqhd,bkhd->bhqk', q_i, k).astype(jnp.float32) * HEAD_DIM ** -0.5
        logits = logits + c_i[..., :, None] - c[..., None, :]
        q_pos = i * Q_BLOCK + jnp.arange(Q_BLOCK)
        logits = jnp.where(k_pos[None, :] <= q_pos[:, None], logits, -jnp.inf)
        p = jax.nn.softmax(logits, axis=-1).astype(v.dtype)
        return jnp.einsum('bhqk,bkhd->bqhd', p, v)

    out = lax.map(block, (q_blocks, c_blocks, jnp.arange(nb)))
    return jnp.moveaxis(out, 0, 1).reshape(b, s, h, dh)


def fox_decode(q, k, v, logf, k_past, v_past, logf_past):
    t = q.shape[1]
    n_past = k_past.shape[1]
    lp = logf_past.astype(jnp.float32)
    suffix = lax.cumsum(lp, axis=1, reverse=True) - lp
    c_new = jnp.cumsum(logf, axis=1).transpose(0, 2, 1)
    scale = HEAD_DIM ** -0.5
    s_past = (jnp.einsum('bthd,bshd->bhts', q, k_past).astype(jnp.float32) * scale
              + suffix.transpose(0, 2, 1)[:, :, None, :] + c_new[..., :, None])
    s_new = (jnp.einsum('bthd,bshd->bhts', q, k).astype(jnp.float32) * scale
             + c_new[..., :, None] - c_new[..., None, :])
    causal = jnp.tril(jnp.ones((t, t), dtype=bool))
    s_new = jnp.where(causal, s_new, -jnp.inf)
    p = jax.nn.softmax(jnp.concatenate([s_past, s_new], axis=-1), axis=-1).astype(v.dtype)
    return (jnp.einsum('bhts,bshd->bthd', p[..., :n_past], v_past)
            + jnp.einsum('bhts,bshd->bthd', p[..., n_past:], v))


def wkv7_scan(s0, r, w, k, v, a, b):
    def to_time(t_):
        return jnp.moveaxis(t_.astype(jnp.float32), 1, 0)

    def step(s, inp):
        r_t, w_t, k_t, v_t, a_t, b_t = inp
        sa = jnp.einsum('bhvk,bhk->bhv', s, a_t)
        s = s * w_t[:, :, None, :] + sa[..., None] * b_t[:, :, None, :] + v_t[..., None] * k_t[:, :, None, :]
        return s, jnp.einsum('bhvk,bhk->bhv', s, r_t)

    s_final, y = lax.scan(step, s0.astype(jnp.float32),
                          (to_time(r), to_time(w), to_time(k), to_time(v), to_time(a), to_time(b)))
    return jnp.moveaxis(y, 0, 1), s_final


def rwkv7_group(p, prev, wkv0, mu, w_decay0, w_decay_lora, w_iclr0, w_iclr_lora, w_gate_lora,
                k_k, k_a, r_k, lnx_g, lnx_b):
    bsz, t, _ = p.shape
    p_prev = jnp.concatenate([prev[:, None].astype(p.dtype), p[:, :-1]], axis=1)
    xs = p + mu * (p_prev - p)
    rw = RWKV_WIDTH
    r, k, v, xw, xa, xg = jnp.split(
        xs, [rw, 2 * rw, 3 * rw, 3 * rw + DECAY_LORA, 3 * rw + DECAY_LORA + ICLR_LORA], axis=-1)
    w = -jax.nn.softplus(-(w_decay0 + jnp.tanh(xw) @ w_decay_lora)) - 0.5
    decay = jnp.exp(-jnp.exp(w.astype(jnp.float32)))
    a = jax.nn.sigmoid(w_iclr0 + xa @ w_iclr_lora)
    g = jax.nn.sigmoid(xg) @ w_gate_lora

    def heads(x_):
        return x_.reshape(bsz, t, H_RWKV, HEAD_DIM)

    kk = heads(k * k_k).astype(jnp.float32)
    kk = kk / jnp.maximum(jnp.sqrt(jnp.sum(jnp.square(kk), axis=-1, keepdims=True)), 1e-12)
    k = k * (1 + (a - 1) * k_a)
    r_h, k_h, v_h, a_h = heads(r), heads(k), heads(v), heads(a)
    y, wkv = wkv7_scan(wkv0, r_h, heads(decay), k_h, v_h, -kk, kk * a_h.astype(jnp.float32))
    m = jnp.mean(y, axis=-1, keepdims=True)
    var = jnp.mean(jnp.square(y - m), axis=-1, keepdims=True)
    yn = ((y - m) * lax.rsqrt(var + GN_EPS)).reshape(bsz, t, rw) * lnx_g + lnx_b
    bonus = (jnp.sum(r_h * k_h * r_k, axis=-1, keepdims=True) * v_h).reshape(bsz, t, rw)
    out = ((yn + bonus) * g).astype(p.dtype)
    return out, p[:, -1], wkv


def trunk(x, c, attend, shift0, wkv0, w_ada, b_ada, w_in, b_f, mu_shift, w_decay0, w_decay_lora,
          w_iclr0, w_iclr_lora, w_gate_lora, k_k, k_a, r_k, lnx_g, lnx_b, w_o, w_ffn_up, w_ffn_down,
          ln_g, ln_b):
    bsz, t, _ = x.shape
    k_rows, v_rows, f_rows, wkv_states, shift_states = [], [], [], [], []
    c_act = jax.nn.silu(c)
    for l in range(DEPTH):
        mod = (c_act @ w_ada[l] + b_ada[l]).reshape(bsz, N_MOD, D_MODEL)[:, None]
        shift = [mod[:, :, 3 * i] for i in range(3)]
        scale = [mod[:, :, 3 * i + 1] for i in range(3)]
        gate = [1 + mod[:, :, 3 * i + 2] for i in range(3)]

        h = x * (1 + scale[0]) + shift[0]
        x = layer_norm(DN_ALPHA * x + 0.5 * gate[0] * swiglu(h, w_ffn_up[l, 0], w_ffn_down[l, 0]),
                       ln_g[l, 0], ln_b[l, 0])

        h = x * (1 + scale[1]) + shift[1]
        proj = h @ w_in[l]
        p_att, p_rwkv = proj[..., :P_ATT], proj[..., P_ATT:]
        q = p_att[..., :ATT_WIDTH].reshape(bsz, t, H_ATT, HEAD_DIM)
        k = p_att[..., ATT_WIDTH:2 * ATT_WIDTH].reshape(bsz, t, H_ATT, HEAD_DIM)
        v = p_att[..., 2 * ATT_WIDTH:3 * ATT_WIDTH].reshape(bsz, t, H_ATT, HEAD_DIM)
        logf = jax.nn.log_sigmoid((p_att[..., 3 * ATT_WIDTH:] + b_f[l]).astype(jnp.float32))
        o_att = attend(l, q, k, v, logf).reshape(bsz, t, ATT_WIDTH)
        o_rwkv, shift_new, wkv_new = rwkv7_group(
            p_rwkv, shift0[l], wkv0[l], mu_shift[l], w_decay0[l], w_decay_lora[l], w_iclr0[l],
            w_iclr_lora[l], w_gate_lora[l], k_k[l], k_a[l], r_k[l], lnx_g[l], lnx_b[l])
        mix = jnp.concatenate([o_att.astype(x.dtype), o_rwkv], axis=-1) @ w_o[l]
        x = layer_norm(DN_ALPHA * x + gate[1] * mix, ln_g[l, 1], ln_b[l, 1])

        h = x * (1 + scale[2]) + shift[2]
        x = layer_norm(DN_ALPHA * x + 0.5 * gate[2] * swiglu(h, w_ffn_up[l, 1], w_ffn_down[l, 1]),
                       ln_g[l, 2], ln_b[l, 2])

        k_rows.append(k)
        v_rows.append(v)
        f_rows.append(logf)
        wkv_states.append(wkv_new)
        shift_states.append(shift_new)
    return (x, jnp.stack(k_rows), jnp.stack(v_rows), jnp.stack(f_rows),
            jnp.stack(wkv_states), jnp.stack(shift_states))


def setup_inputs(seed: int = 0) -> dict:
    key = jax.random.key(seed)
    ks = jax.random.split(key, 32)
    n_pages = PAST_LEN // PAGE_SIZE
    n_used = DEC_BATCH * n_pages
    n_pool = (5 * n_used) // 4

    def nrm(k_, shape, s=1.0):
        return s * jax.random.normal(k_, shape, jnp.float32)

    page_table = jax.random.permutation(ks[7], n_pool)[:n_used].reshape(DEC_BATCH, n_pages).astype(jnp.int32)
    return {
        'x_prompt': nrm(ks[0], (BATCH, SEQ, D_MODEL)),
        'x_sample': nrm(ks[1], (DEC_BATCH, DEC_SEQ, D_MODEL)),
        'cache_k': nrm(ks[2], (DEPTH, n_pool, PAGE_SIZE, H_ATT, HEAD_DIM)),
        'cache_v': nrm(ks[3], (DEPTH, n_pool, PAGE_SIZE, H_ATT, HEAD_DIM)),
        'cache_logf': jax.nn.log_sigmoid(FORGET_BIAS + nrm(ks[4], (DEPTH, n_pool, PAGE_SIZE, H_ATT))),
        'state_wkv': nrm(ks[5], (DEPTH, DEC_BATCH, H_RWKV, HEAD_DIM, HEAD_DIM), 0.3),
        'state_shift': nrm(ks[6], (DEPTH, DEC_BATCH, P_RWKV)),
        'page_table': page_table,
        'c_prompt': nrm(ks[8], (BATCH, D_MODEL)),
        'c_sample': nrm(ks[9], (DEC_BATCH, D_MODEL)),
        'w_ada': nrm(ks[10], (DEPTH, D_MODEL, N_MOD * D_MODEL), 0.5 * D_MODEL ** -0.5),
        'b_ada': nrm(ks[11], (DEPTH, N_MOD * D_MODEL), 0.01),
        'w_in': nrm(ks[12], (DEPTH, D_MODEL, P_IN), D_MODEL ** -0.5),
        'b_f': FORGET_BIAS + nrm(ks[13], (DEPTH, H_ATT), 0.1),
        'mu_shift': jax.random.uniform(ks[14], (DEPTH, P_RWKV), jnp.float32),
        'w_decay0': nrm(ks[15], (DEPTH, RWKV_WIDTH), 0.5),
        'w_decay_lora': nrm(ks[16], (DEPTH, DECAY_LORA, RWKV_WIDTH), 0.5 * DECAY_LORA ** -0.5),
        'w_iclr0': nrm(ks[17], (DEPTH, RWKV_WIDTH), 0.5),
        'w_iclr_lora': nrm(ks[18], (DEPTH, ICLR_LORA, RWKV_WIDTH), 0.5 * ICLR_LORA ** -0.5),
        'w_gate_lora': nrm(ks[19], (DEPTH, GATE_LORA, RWKV_WIDTH), GATE_LORA ** -0.5),
        'k_k': 0.85 + nrm(ks[20], (DEPTH, RWKV_WIDTH), 0.1),
        'k_a': 1.0 + nrm(ks[21], (DEPTH, RWKV_WIDTH), 0.1),
        'r_k': nrm(ks[22], (DEPTH, H_RWKV, HEAD_DIM), 0.1),
        'lnx_g': 1.0 + nrm(ks[23], (DEPTH, RWKV_WIDTH), 0.1),
        'lnx_b': nrm(ks[24], (DEPTH, RWKV_WIDTH), 0.01),
        'w_o': nrm(ks[25], (DEPTH, D_MODEL, D_MODEL), DN_BETA * D_MODEL ** -0.5),
        'w_ffn_up': nrm(ks[26], (DEPTH, 2, D_MODEL, 2 * D_FF), D_MODEL ** -0.5),
        'w_ffn_down': nrm(ks[27], (DEPTH, 2, D_FF, D_MODEL), DN_BETA * D_FF ** -0.5),
        'ln_g': 1.0 + nrm(ks[28], (DEPTH, 3, D_MODEL), 0.05),
        'ln_b': nrm(ks[29], (DEPTH, 3, D_MODEL), 0.01),
    }


def reference(x_prompt, x_sample, cache_k, cache_v, cache_logf, state_wkv, state_shift, page_table,
              c_prompt, c_sample, w_ada, b_ada, w_in, b_f, mu_shift, w_decay0, w_decay_lora, w_iclr0,
              w_iclr_lora, w_gate_lora, k_k, k_a, r_k, lnx_g, lnx_b, w_o, w_ffn_up, w_ffn_down, ln_g, ln_b):
    params = (w_ada, b_ada, w_in, b_f, mu_shift, w_decay0, w_decay_lora, w_iclr0, w_iclr_lora,
              w_gate_lora, k_k, k_a, r_k, lnx_g, lnx_b, w_o, w_ffn_up, w_ffn_down, ln_g, ln_b)
    n_prompt = x_prompt.shape[0]
    n_dec = x_sample.shape[0]

    def attend_prompt(l, q, k, v, logf):
        return fox_prompt(q, k, v, logf)

    def attend_sample(l, q, k, v, logf):
        k_past = cache_k[l][page_table].reshape(n_dec, -1, H_ATT, HEAD_DIM)
        v_past = cache_v[l][page_table].reshape(n_dec, -1, H_ATT, HEAD_DIM)
        f_past = cache_logf[l][page_table].reshape(n_dec, -1, H_ATT)
        return fox_decode(q, k, v, logf, k_past, v_past, f_past)

    (y_prompt, k_prompt, v_prompt, logf_prompt, wkv_prompt, shift_prompt) = trunk(
        x_prompt, c_prompt, attend_prompt,
        jnp.zeros((DEPTH, n_prompt, P_RWKV), x_prompt.dtype),
        jnp.zeros((DEPTH, n_prompt, H_RWKV, HEAD_DIM, HEAD_DIM), jnp.float32),
        *params)
    (y_sample, k_sample, v_sample, logf_sample, wkv_sample, shift_sample) = trunk(
        x_sample, c_sample, attend_sample, state_shift, state_wkv, *params)
    return (y_prompt, y_sample, k_prompt, v_prompt, logf_prompt, wkv_prompt, shift_prompt,
            k_sample, v_sample, logf_sample, wkv_sample, shift_sample)
```

```python
import functools

import jax
import jax.numpy as jnp
from jax import lax
from jax.experimental import pallas as pl
from jax.experimental.pallas import tpu as pltpu

F32 = jnp.float32
BF16 = jnp.bfloat16

HEAD_DIM = 64
LANES = 128
LN_EPS = 1e-5
GN_EPS = 64e-5
NEG_BIG = -1e30
VMEM_CAP = 60 * 1024 * 1024


def _cparams(sem, est_bytes):
    limit = int(min(VMEM_CAP, max(32 * 1024 * 1024, est_bytes)))
    return pltpu.CompilerParams(dimension_semantics=sem, vmem_limit_bytes=limit)


def _dot(a, b):
    return jnp.dot(a, b, preferred_element_type=F32)


def _dot_nt(a, b):
    return lax.dot_general(a, b, (((1,), (1,)), ((), ())), preferred_element_type=F32)


def _split2(x):
    hi = x.astype(BF16)
    lo = (x - hi.astype(F32)).astype(BF16)
    return hi, lo


def _split3(x):
    hi = x.astype(BF16)
    r1 = x - hi.astype(F32)
    mid = r1.astype(BF16)
    lo = (r1 - mid.astype(F32)).astype(BF16)
    return hi, mid, lo


def _dot_exact_rhs(parts, m):
    acc = _dot(parts[0], m)
    for p in parts[1:]:
        acc = acc + _dot(p, m)
    return acc


def _sigmoid(x):
    return 1.0 / (1.0 + jnp.exp(-x))


def _softplus(x):
    return jnp.maximum(x, 0.0) + jnp.log1p(jnp.exp(-jnp.abs(x)))


def _layer_norm(y, g, b):
    mu = jnp.mean(y, axis=-1, keepdims=True)
    d = y - mu
    var = jnp.mean(d * d, axis=-1, keepdims=True)
    return d * lax.rsqrt(var + LN_EPS) * g + b


def _seg_ones(n):
    r = lax.broadcasted_iota(jnp.int32, (n, n), 0) // HEAD_DIM
    c = lax.broadcasted_iota(jnp.int32, (n, n), 1) // HEAD_DIM
    return (r == c).astype(BF16)


def _ada_kernel(c_ref, w_ref, b_ref, o_ref):
    c = c_ref[...]
    ca = (c * _sigmoid(c)).astype(BF16)
    o_ref[...] = _dot(ca, w_ref[...].astype(BF16)) + b_ref[...]


def _ada(c_all, w_ada, b_ada):
    depth, d, n = w_ada.shape
    m = c_all.shape[0]
    tn = n // 4 if (n // 4) % LANES == 0 and n % 4 == 0 else n
    est = 2 * (d * tn * 4 + m * tn * 4 + m * d * 4) + d * tn * 2 + (4 << 20)
    return pl.pallas_call(
        _ada_kernel,
        out_shape=jax.ShapeDtypeStruct((depth, m, n), F32),
        grid=(depth, n // tn),
        in_specs=[
            pl.BlockSpec((m, d), lambda l, j: (0, 0)),
            pl.BlockSpec((None, d, tn), lambda l, j: (l, 0, j)),
            pl.BlockSpec((None, 1, tn), lambda l, j: (l, 0, j)),
        ],
        out_specs=pl.BlockSpec((None, m, tn), lambda l, j: (l, 0, j)),
        compiler_params=_cparams(("arbitrary", "arbitrary"), est),
        name="ada_mod",
    )(c_all, w_ada, b_ada.reshape(depth, 1, n))


def _ffn_kernel(x_ref, sh_ref, sc_ref, gt_ref, wg_ref, wu_ref, wd_ref, g_ref, b_ref, o_ref,
                h_scr, acc_scr, *, alpha):
    j = pl.program_id(1)

    @pl.when(j == 0)
    def _():
        h_scr[...] = (x_ref[...] * (1.0 + sc_ref[0]) + sh_ref[0]).astype(BF16)
        acc_scr[...] = jnp.zeros_like(acc_scr)

    h = h_scr[...]
    gate = _dot(h, wg_ref[...].astype(BF16))
    up = _dot(h, wu_ref[...].astype(BF16))
    act = (gate * _sigmoid(gate) * up).astype(BF16)
    acc_scr[...] += _dot(act, wd_ref[...].astype(BF16))

    @pl.when(j == pl.num_programs(1) - 1)
    def _():
        y = alpha * x_ref[...] + 0.5 * (1.0 + gt_ref[0]) * acc_scr[...]
        o_ref[...] = _layer_norm(y, g_ref[...], b_ref[...])


def _mod_spec(mod, rows_per_mod, tm):
    blk = (1,) + mod.shape[1:]
    if mod.shape[0] == 1:
        return pl.BlockSpec(blk, lambda i, j: (0, 0, 0))
    tiles = rows_per_mod // tm
    return pl.BlockSpec(blk, lambda i, j: (i // tiles, 0, 0))


def _ffn(x, shift, scale, gate, w_up, w_down, ln_g, ln_b, l, s, *, tm, rows_per_mod, alpha):
    r, d = x.shape
    f = w_down.shape[2]
    tf = 256 if f % 256 == 0 else f
    nf = f // tf
    est = (2 * (2 * tm * d * 4 + 3 * d * tf * 4) + tm * d * 6 + 6 * tm * tf * 4 + 3 * d * tf * 2
           + (4 << 20))
    kern = functools.partial(_ffn_kernel, alpha=alpha)
    return pl.pallas_call(
        kern,
        out_shape=jax.ShapeDtypeStruct((r, d), F32),
        grid=(r // tm, nf),
        in_specs=[
            pl.BlockSpec((tm, d), lambda i, j: (i, 0)),
            _mod_spec(shift, rows_per_mod, tm),
            _mod_spec(scale, rows_per_mod, tm),
            _mod_spec(gate, rows_per_mod, tm),
            pl.BlockSpec((None, None, d, tf), lambda i, j: (l, s, 0, j)),
            pl.BlockSpec((None, None, d, tf), lambda i, j: (l, s, 0, j + nf)),
            pl.BlockSpec((None, None, tf, d), lambda i, j: (l, s, j, 0)),
            pl.BlockSpec((None, None, 1, d), lambda i, j: (l, 2 * s, 0, 0)),
            pl.BlockSpec((None, None, 1, d), lambda i, j: (l, 2 * s, 0, 0)),
        ],
        out_specs=pl.BlockSpec((tm, d), lambda i, j: (i, 0)),
        scratch_shapes=[pltpu.VMEM((tm, d), BF16), pltpu.VMEM((tm, d), F32)],
        compiler_params=_cparams(("arbitrary", "arbitrary"), est),
        name="ffn",
    )(x, shift, scale, gate, w_up, w_up, w_down, ln_g, ln_b)


def _proj_kernel(x_ref, sh_ref, sc_ref, w_ref, o_ref, h_scr):
    @pl.when(pl.program_id(1) == 0)
    def _():
        h_scr[...] = (x_ref[...] * (1.0 + sc_ref[0]) + sh_ref[0]).astype(BF16)

    o_ref[...] = _dot(h_scr[...], w_ref[...].astype(BF16))


def _proj_logf_kernel(x_ref, sh_ref, sc_ref, w_ref, wf_ref, bfr_ref, bfc_ref, o_ref, lf_ref, lft_ref,
                      h_scr):
    @pl.when(pl.program_id(1) == 0)
    def _():
        h = (x_ref[...] * (1.0 + sc_ref[0]) + sh_ref[0]).astype(BF16)
        h_scr[...] = h
        wf = wf_ref[...].astype(BF16)
        z = _dot_nt(h, wf) + bfr_ref[...]
        lf_ref[...] = -_softplus(-z)
        zt = _dot_nt(wf, h) + bfc_ref[...]
        lft_ref[...] = -_softplus(-zt)

    o_ref[...] = _dot(h_scr[...], w_ref[...].astype(BF16))


def _proj(x, shift, scale, w, l, n, *, tm, rows_per_mod, logf=None):
    r, d = x.shape
    tn = n // 2 if (n // 2) % LANES == 0 else n
    est = 2 * (tm * d * 4 + d * tn * 4 + tm * tn * 4) + tm * d * 2 + d * tn * 2 + tm * tn * 4 + (4 << 20)
    in_specs = [
        pl.BlockSpec((tm, d), lambda i, j: (i, 0)),
        _mod_spec(shift, rows_per_mod, tm),
        _mod_spec(scale, rows_per_mod, tm),
        pl.BlockSpec((None, d, tn), lambda i, j: (l, 0, j)),
    ]
    o_spec = pl.BlockSpec((tm, tn), lambda i, j: (i, j))
    o_shape = jax.ShapeDtypeStruct((r, n), F32)
    if logf is None:
        return pl.pallas_call(
            _proj_kernel, out_shape=o_shape, grid=(r // tm, n // tn), in_specs=in_specs,
            out_specs=o_spec, scratch_shapes=[pltpu.VMEM((tm, d), BF16)],
            compiler_params=_cparams(("arbitrary", "arbitrary"), est), name="proj",
        )(x, shift, scale, w)
    w_ft, b_f = logf
    h = w_ft.shape[1]
    in_specs += [
        pl.BlockSpec((None, h, d), lambda i, j: (l, 0, 0)),
        pl.BlockSpec((None, 1, h), lambda i, j: (l, 0, 0)),
        pl.BlockSpec((None, h, 1), lambda i, j: (l, 0, 0)),
    ]
    return pl.pallas_call(
        _proj_logf_kernel,
        out_shape=(o_shape, jax.ShapeDtypeStruct((r, h), F32), jax.ShapeDtypeStruct((h, r), F32)),
        grid=(r // tm, n // tn), in_specs=in_specs,
        out_specs=(o_spec, pl.BlockSpec((tm, h), lambda i, j: (i, 0)),
                   pl.BlockSpec((h, tm), lambda i, j: (0, i))),
        scratch_shapes=[pltpu.VMEM((tm, d), BF16)],
        compiler_params=_cparams(("arbitrary", "arbitrary"), est), name="proj_logf",
    )(x, shift, scale, w, w_ft, b_f.reshape(-1, 1, h), b_f.reshape(-1, h, 1))


def _fcum_kernel(lf_ref, lft_ref, c_ref, ct_ref, *, cb):
    s, h = lf_ref.shape
    r = lax.broadcasted_iota(jnp.int32, (cb, cb), 0)
    c = lax.broadcasted_iota(jnp.int32, (cb, cb), 1)
    low = (c <= r).astype(BF16)
    upp = (r <= c).astype(BF16)
    carry = jnp.zeros((1, h), F32)
    carry_t = jnp.zeros((h, 1), F32)
    for blk in range(s // cb):
        x = lf_ref[blk * cb:(blk + 1) * cb, :]
        xs = _split3(x)
        cs = _dot(low, xs[0]) + _dot(low, xs[1]) + _dot(low, xs[2]) + carry
        c_ref[blk * cb:(blk + 1) * cb, :] = cs
        carry = cs[cb - 1:cb, :]
        xt = lft_ref[:, blk * cb:(blk + 1) * cb]
        cst = _dot_exact_rhs(_split3(xt), upp) + carry_t
        ct_ref[:, blk * cb:(blk + 1) * cb] = cst
        carry_t = cst[:, cb - 1:cb]


def _fcum(lf, lft, nb, s):
    h = lf.shape[1]
    cb = min(256, s)
    return pl.pallas_call(
        functools.partial(_fcum_kernel, cb=cb),
        out_shape=(jax.ShapeDtypeStruct(lf.shape, F32), jax.ShapeDtypeStruct(lft.shape, F32)),
        grid=(nb,),
        in_specs=[pl.BlockSpec((s, h), lambda b: (b, 0)), pl.BlockSpec((h, s), lambda b: (0, b))],
        out_specs=(pl.BlockSpec((s, h), lambda b: (b, 0)), pl.BlockSpec((h, s), lambda b: (0, b))),
        compiler_params=_cparams(("arbitrary",), 0),
        name="fcum",
    )(lf, lft)


def _fox_prompt_kernel(q_ref, k_ref, v_ref, c_ref, ct_ref, o_ref, kb_scr, vb_scr, *, tq):
    hp = pl.program_id(1)
    i = pl.program_id(2)

    @pl.when(i == 0)
    def _():
        kb_scr[...] = k_ref[...].astype(BF16)
        vb_scr[...] = v_ref[...].astype(BF16)

    q = q_ref[...] * (HEAD_DIM ** -0.5)
    lane = lax.broadcasted_iota(jnp.int32, (1, LANES), 1)
    hlane = lax.broadcasted_iota(jnp.int32, (1, c_ref.shape[1]), 1)
    row = lax.broadcasted_iota(jnp.int32, (tq, tq), 0)
    col = lax.broadcasted_iota(jnp.int32, (tq, tq), 1)
    out = jnp.zeros((tq, LANES), F32)
    for hh in range(2):
        head = 2 * hp + hh
        in_half = (lane // HEAD_DIM) == hh
        qm = jnp.where(in_half, q, 0.0).astype(BF16)
        c_i = jnp.sum(jnp.where(hlane == head, c_ref[...], 0.0), axis=-1, keepdims=True)

        def body(kj, carry, qm=qm, c_i=c_i, head=head):
            m, l, acc = carry
            off = pl.multiple_of(kj * tq, tq)
            kblk = kb_scr[pl.ds(off, tq), :]
            vblk = vb_scr[pl.ds(off, tq), :]
            s = _dot_nt(qm, kblk) + c_i - ct_ref[pl.ds(head, 1), pl.ds(off, tq)]
            s = jnp.where(col <= row + (i - kj) * tq, s, NEG_BIG)
            m_new = jnp.maximum(m, jnp.max(s, axis=-1, keepdims=True))
            p = jnp.exp(s - m_new)
            a = jnp.exp(m - m_new)
            l = a * l + jnp.sum(p, axis=-1, keepdims=True)
            acc = a * acc + _dot(p.astype(BF16), vblk)
            return m_new, l, acc

        init = (jnp.full((tq, 1), NEG_BIG, F32), jnp.zeros((tq, 1), F32), jnp.zeros((tq, LANES), F32))
        _, l, acc = lax.fori_loop(0, i + 1, body, init)
        out = jnp.where(in_half, acc / l, out)
    o_ref[...] = out


def _fox_prompt(qkv, c, ct, nb, s):
    r = qkv.shape[0]
    aw = qkv.shape[1] // 3
    nhp = aw // LANES
    tq = min(256, s)
    nq = s // tq
    h = c.shape[1]
    est = 2 * (2 * s * LANES * 4 + 2 * tq * LANES * 4 + tq * LANES * 4 + 8 * s * 4) + 2 * s * LANES * 2 \
        + 8 * tq * tq * 4 + (4 << 20)
    return pl.pallas_call(
        functools.partial(_fox_prompt_kernel, tq=tq),
        out_shape=jax.ShapeDtypeStruct((r, aw), F32),
        grid=(nb, nhp, nq),
        in_specs=[
            pl.BlockSpec((tq, LANES), lambda b, p, i: (b * nq + i, p)),
            pl.BlockSpec((s, LANES), lambda b, p, i: (b, nhp + p)),
            pl.BlockSpec((s, LANES), lambda b, p, i: (b, 2 * nhp + p)),
            pl.BlockSpec((tq, h), lambda b, p, i: (b * nq + i, 0)),
            pl.BlockSpec((h, s), lambda b, p, i: (0, b)),
        ],
        out_specs=pl.BlockSpec((tq, LANES), lambda b, p, i: (b * nq + i, p)),
        scratch_shapes=[pltpu.VMEM((s, LANES), BF16), pltpu.VMEM((s, LANES), BF16)],
        compiler_params=_cparams(("arbitrary", "arbitrary", "arbitrary"), est),
        name="fox_prompt",
    )(qkv, qkv, qkv, c, ct)


def _fox_decode_kernel(pt_ref, qkv_ref, lf_ref, kc_ref, vc_ref, fc_ref, o_ref,
                       m_scr, l_scr, o_scr, tot_scr, *, n_pages, aw):
    j = pl.program_id(1)
    ps = kc_ref.shape[0]
    nh = aw // HEAD_DIM
    seg = (lax.broadcasted_iota(jnp.int32, (aw, nh), 0) // HEAD_DIM
           == lax.broadcasted_iota(jnp.int32, (aw, nh), 1)).astype(BF16)
    seg_t = (lax.broadcasted_iota(jnp.int32, (nh, aw), 1) // HEAD_DIM
             == lax.broadcasted_iota(jnp.int32, (nh, aw), 0)).astype(BF16)

    @pl.when(j == 0)
    def _():
        tot_scr[...] = jnp.zeros_like(tot_scr)

    q = qkv_ref[0, :, 0:aw] * (HEAD_DIM ** -0.5)
    lf_new = lf_ref[0]
    lp = fc_ref[...]
    r = lax.broadcasted_iota(jnp.int32, (ps, ps), 0)
    c = lax.broadcasted_iota(jnp.int32, (ps, ps), 1)
    upper = (c > r).astype(BF16)
    lps = _split3(lp)
    suffix = _dot(upper, lps[0]) + _dot(upper, lps[1]) + _dot(upper, lps[2]) + tot_scr[...]
    tot_scr[...] = tot_scr[...] + jnp.sum(lp, axis=0, keepdims=True)

    logits = _dot((kc_ref[...] * q).astype(BF16), seg) + suffix + lf_new
    m = jnp.max(logits, axis=0, keepdims=True)
    p = jnp.exp(logits - m)
    pe = _dot(p.astype(BF16), seg_t)
    o_scr[pl.ds(j, 1), :] = jnp.sum(pe * vc_ref[...], axis=0, keepdims=True)
    m_scr[pl.ds(j, 1), :] = m
    l_scr[pl.ds(j, 1), :] = jnp.sum(p, axis=0, keepdims=True)

    @pl.when(j == n_pages - 1)
    def _():
        k_new = qkv_ref[0, :, aw:2 * aw]
        v_new = qkv_ref[0, :, 2 * aw:3 * aw]
        s_new = _dot((q * k_new).astype(BF16), seg)
        m_all = jnp.maximum(jnp.max(m_scr[...], axis=0, keepdims=True), s_new)
        wj = jnp.exp(m_scr[...] - m_all)
        p_new = jnp.exp(s_new - m_all)
        l_all = jnp.sum(wj * l_scr[...], axis=0, keepdims=True) + p_new
        seg_f = seg_t.astype(F32)
        hi = lax.Precision.HIGHEST
        wje = jnp.dot(wj, seg_f, precision=hi, preferred_element_type=F32)
        pne = jnp.dot(p_new, seg_f, precision=hi, preferred_element_type=F32)
        le = jnp.dot(l_all, seg_f, precision=hi, preferred_element_type=F32)
        num = jnp.sum(wje * o_scr[...], axis=0, keepdims=True) + pne * v_new
        o_ref[0] = num / le


def _fox_decode(qkv, lf, page_table, cache_k, cache_v, cache_logf, l):
    n, n_pages = page_table.shape
    _, n_pool, ps, nh, dh = cache_k.shape
    aw = nh * dh
    kc = cache_k.reshape(cache_k.shape[0], n_pool, ps, aw)
    vc = cache_v.reshape(cache_v.shape[0], n_pool, ps, aw)
    pt = page_table.reshape(-1)

    def page(i, j, pt_ref):
        return pt_ref[i * n_pages + (n_pages - 1 - j)]

    grid_spec = pltpu.PrefetchScalarGridSpec(
        num_scalar_prefetch=1,
        grid=(n, n_pages),
        in_specs=[
            pl.BlockSpec((1, 1, 3 * aw), lambda i, j, pt_ref: (i, 0, 0)),
            pl.BlockSpec((1, 1, nh), lambda i, j, pt_ref: (i, 0, 0)),
            pl.BlockSpec((None, None, ps, aw), lambda i, j, pt_ref: (l, page(i, j, pt_ref), 0, 0)),
            pl.BlockSpec((None, None, ps, aw), lambda i, j, pt_ref: (l, page(i, j, pt_ref), 0, 0)),
            pl.BlockSpec((None, None, ps, nh), lambda i, j, pt_ref: (l, page(i, j, pt_ref), 0, 0)),
        ],
        out_specs=pl.BlockSpec((1, 1, aw), lambda i, j, pt_ref: (i, 0, 0)),
        scratch_shapes=[pltpu.VMEM((n_pages, nh), F32), pltpu.VMEM((n_pages, nh), F32),
                        pltpu.VMEM((n_pages, aw), F32), pltpu.VMEM((1, nh), F32)],
    )
    out = pl.pallas_call(
        functools.partial(_fox_decode_kernel, n_pages=n_pages, aw=aw),
        out_shape=jax.ShapeDtypeStruct((n, 1, aw), F32),
        grid_spec=grid_spec,
        compiler_params=_cparams(("arbitrary", "arbitrary"), 0),
        name="fox_decode",
    )(pt, qkv.reshape(n, 1, 3 * aw), lf.reshape(n, 1, nh), kc, vc, cache_logf)
    return out.reshape(n, aw)


def _rwkv_prep_kernel(p_ref, prev_ref, mu_ref, wd0_ref, wi0_ref, wwa_ref, wg_ref, kk_ref, ka_ref, rk_ref,
                      r_o, w_o, k_o, v_o, kk_o, b_o, g_o, bonus_o, *, rw, seq_tiles, per_row_prev):
    p = p_ref[...]
    tm = p.shape[0]
    if per_row_prev:
        p_prev = prev_ref[...]
    else:
        first = (pl.program_id(0) % seq_tiles) == 0
        prev8 = prev_ref[...]
        before = jnp.where(first, 0.0, prev8[7:8, :])
        rolled = pltpu.roll(p, 1, 0)
        rowi = lax.broadcasted_iota(jnp.int32, (tm, 1), 0)
        p_prev = jnp.where(rowi == 0, before, rolled)
    xs = p + mu_ref[...] * (p_prev - p)
    r = xs[:, 0:rw]
    k = xs[:, rw:2 * rw]
    v = xs[:, 2 * rw:3 * rw]
    nwa = wwa_ref.shape[0]
    dl = nwa // 2
    wa = xs[:, 3 * rw:3 * rw + nwa]
    lane = lax.broadcasted_iota(jnp.int32, (1, nwa), 1)
    wa_t = jnp.where(lane < dl, jnp.tanh(wa), wa).astype(BF16)
    lora = _dot(wa_t, wwa_ref[...].astype(BF16))
    w = -_softplus(-(wd0_ref[...] + lora[:, 0:rw])) - 0.5
    decay = jnp.exp(-jnp.exp(w))
    a = _sigmoid(wi0_ref[...] + lora[:, rw:2 * rw])
    xg = xs[:, 3 * rw + nwa:]
    g = _dot(_sigmoid(xg).astype(BF16), wg_ref[...].astype(BF16))
    seg = _seg_ones(rw)
    kk = k * kk_ref[...]
    ss = _dot_exact_rhs(_split2(kk * kk), seg)
    kk = kk / jnp.maximum(jnp.sqrt(ss), 1e-12)
    k2 = k * (1.0 + (a - 1.0) * ka_ref[...])
    bonus = _dot_exact_rhs(_split2(r * k2 * rk_ref[...]), seg) * v
    r_o[...] = r
    w_o[...] = decay
    k_o[...] = k2
    v_o[...] = v
    kk_o[...] = kk
    b_o[...] = kk * a
    g_o[...] = g
    bonus_o[...] = bonus


def _rwkv_prep(p, prev, mu, wd0, wi0, wwa, wg, k_k, k_a, r_k, l, *, tm, seq_tiles, per_row_prev):
    r, pw = p.shape
    rw = wd0.shape[-1]
    nwa = wwa.shape[1]
    ng = wg.shape[1]
    if per_row_prev:
        prev_spec = pl.BlockSpec((tm, pw), lambda i: (i, 0))
    else:
        prev_spec = pl.BlockSpec((8, pw), lambda i: (jnp.maximum(i * (tm // 8) - 1, 0), 0))

    def vec(n):
        return pl.BlockSpec((None, 1, n), lambda i: (l, 0, 0))

    o_spec = pl.BlockSpec((tm, rw), lambda i: (i, 0))
    o_shape = jax.ShapeDtypeStruct((r, rw), F32)
    est = 2 * (tm * pw * 4 + 8 * tm * rw * 4) + 12 * tm * rw * 4 + 3 * tm * pw * 4 + (6 << 20)
    return pl.pallas_call(
        functools.partial(_rwkv_prep_kernel, rw=rw, seq_tiles=seq_tiles, per_row_prev=per_row_prev),
        out_shape=(o_shape,) * 8,
        grid=(r // tm,),
        in_specs=[
            pl.BlockSpec((tm, pw), lambda i: (i, 0)), prev_spec, vec(pw), vec(rw), vec(rw),
            pl.BlockSpec((None, nwa, 2 * rw), lambda i: (l, 0, 0)),
            pl.BlockSpec((None, ng, rw), lambda i: (l, 0, 0)),
            vec(rw), vec(rw), vec(rw),
        ],
        out_specs=(o_spec,) * 8,
        compiler_params=_cparams(("arbitrary",), est),
        name="rwkv_prep",
    )(p, prev, mu, wd0, wi0, wwa, wg, k_k, k_a, r_k)


def _wkv_kernel(*refs, nbc, npair, tb, zero_init):
    if zero_init:
        r_ref, w_ref, k_ref, v_ref, kk_ref, b_ref, y_ref, so_ref, s_scr = refs
        s0_ref = None
    else:
        r_ref, w_ref, k_ref, v_ref, kk_ref, b_ref, s0_ref, y_ref, so_ref, s_scr = refs
    tj = pl.program_id(1)
    hd = HEAD_DIM

    @pl.when(tj == 0)
    def _():
        if zero_init:
            s_scr[...] = jnp.zeros_like(s_scr)
        else:
            for bi in range(nbc):
                for pp in range(npair):
                    s_scr[bi, pp] = jnp.concatenate(
                        [s0_ref[bi, 2 * pp], s0_ref[bi, 2 * pp + 1]], axis=-1)

    seg = _seg_ones(LANES)
    eye2 = (lax.broadcasted_iota(jnp.int32, (hd, LANES), 0)
            == lax.broadcasted_iota(jnp.int32, (hd, LANES), 1) % hd).astype(F32)

    grp = 8 if tb % 8 == 0 else tb
    chains = [(bi, pp) for bi in range(nbc) for pp in range(npair)]

    def step_group(tg, carry):
        t0 = 0 if tb == grp else pl.multiple_of(tg * grp, grp)
        rows = {}
        state = {}
        for bi, pp in chains:
            sl = pl.ds(pp * LANES, LANES)
            rows[bi, pp] = [ref[bi, pl.ds(t0, grp), sl]
                            for ref in (r_ref, w_ref, k_ref, v_ref, kk_ref, b_ref)]
            state[bi, pp] = s_scr[bi, pp]
        ys = {c: [] for c in chains}
        for j in range(grp):
            for c in chains:
                r_row, w_row, k_row, v_row, kk_row, b_row = [x[j:j + 1, :] for x in rows[c]]
                s = state[c]
                sa = _dot_exact_rhs(_split2(s * (-kk_row)), seg)
                vb = _dot_exact_rhs(_split2(eye2 * v_row), seg)
                s_new = s * w_row + sa * b_row + vb * k_row
                state[c] = s_new
                yb = _dot((s_new * r_row).astype(BF16), seg)
                ys[c].append(jnp.sum(yb * eye2, axis=0, keepdims=True))
        for bi, pp in chains:
            s_scr[bi, pp] = state[bi, pp]
            y_ref[bi, pl.ds(t0, grp), pl.ds(pp * LANES, LANES)] = jnp.concatenate(ys[bi, pp], axis=0)
        return carry

    if tb == grp:
        step_group(0, 0)
    else:
        lax.fori_loop(0, tb // grp, step_group, 0)

    @pl.when(tj == pl.num_programs(1) - 1)
    def _():
        for bi in range(nbc):
            for pp in range(npair):
                s = s_scr[bi, pp]
                so_ref[bi, 2 * pp] = s[:, 0:hd]
                so_ref[bi, 2 * pp + 1] = s[:, hd:2 * hd]


def _wkv(r, w, k, v, kk, b, state0, nb, t, *, nbc, tb):
    rw = r.shape[1]
    nh = rw // HEAD_DIM
    npair = nh // 2
    ins = [x.reshape(nb, t, rw) for x in (r, w, k, v, kk, b)]
    seq_spec = pl.BlockSpec((nbc, tb, rw), lambda g, j: (g, j, 0))
    st_spec = pl.BlockSpec((nbc, nh, HEAD_DIM, HEAD_DIM), lambda g, j: (g, 0, 0, 0))
    in_specs = [seq_spec] * 6
    if state0 is not None:
        ins.append(state0)
        in_specs.append(st_spec)
    est = 2 * 7 * nbc * tb * rw * 4 + 6 * nbc * nh * HEAD_DIM * LANES * 4 + (6 << 20)
    y, s_out = pl.pallas_call(
        functools.partial(_wkv_kernel, nbc=nbc, npair=npair, tb=tb, zero_init=state0 is None),
        out_shape=(jax.ShapeDtypeStruct((nb, t, rw), F32),
                   jax.ShapeDtypeStruct((nb, nh, HEAD_DIM, HEAD_DIM), F32)),
        grid=(nb // nbc, t // tb),
        in_specs=in_specs,
        out_specs=(seq_spec, st_spec),
        scratch_shapes=[pltpu.VMEM((nbc, npair, HEAD_DIM, LANES), F32)],
        compiler_params=_cparams(("arbitrary", "arbitrary"), est),
        name="wkv_scan",
    )(*ins)
    return y.reshape(nb * t, rw), s_out


def _out_kernel(x_ref, gt_ref, oa_ref, y_ref, g_ref, bonus_ref, lg_ref, lb_ref, wo_ref, ng_ref, nb_ref,
                o_ref, *, alpha, aw):
    y = y_ref[...]
    rw = y.shape[1]
    seg = _seg_ones(rw)
    inv = 1.0 / HEAD_DIM
    m = _dot_exact_rhs(_split2(y), seg) * inv
    d = y - m
    var = _dot_exact_rhs(_split2(d * d), seg) * inv
    yn = d * lax.rsqrt(var + GN_EPS) * lg_ref[...] + lb_ref[...]
    o_rwkv = ((yn + bonus_ref[...]) * g_ref[...]).astype(BF16)
    mix = _dot(oa_ref[...].astype(BF16), wo_ref[0:aw, :].astype(BF16)) \
        + _dot(o_rwkv, wo_ref[aw:, :].astype(BF16))
    z = alpha * x_ref[...] + (1.0 + gt_ref[0]) * mix
    o_ref[...] = _layer_norm(z, ng_ref[...], nb_ref[...])


def _out_stage(x, gate, o_att, y, g, bonus, lnx_g, lnx_b, w_o, ln_g, ln_b, l, *, tm, rows_per_mod, alpha):
    r, d = x.shape
    aw = o_att.shape[1]
    rw = y.shape[1]
    blk = (1,) + gate.shape[1:]
    if gate.shape[0] == 1:
        gate_spec = pl.BlockSpec(blk, lambda i: (0, 0, 0))
    else:
        tiles = rows_per_mod // tm
        gate_spec = pl.BlockSpec(blk, lambda i: (i // tiles, 0, 0))

    def rows(n):
        return pl.BlockSpec((tm, n), lambda i: (i, 0))

    est = 2 * (2 * tm * d * 4 + tm * aw * 4 + 3 * tm * rw * 4 + d * d * 4) + d * d * 2 + 8 * tm * rw * 4 \
        + 3 * tm * d * 4 + (4 << 20)
    return pl.pallas_call(
        functools.partial(_out_kernel, alpha=alpha, aw=aw),
        out_shape=jax.ShapeDtypeStruct((r, d), F32),
        grid=(r // tm,),
        in_specs=[
            rows(d), gate_spec, rows(aw), rows(rw), rows(rw), rows(rw),
            pl.BlockSpec((None, 1, rw), lambda i: (l, 0, 0)),
            pl.BlockSpec((None, 1, rw), lambda i: (l, 0, 0)),
            pl.BlockSpec((None, d, d), lambda i: (l, 0, 0)),
            pl.BlockSpec((None, None, 1, d), lambda i: (l, 1, 0, 0)),
            pl.BlockSpec((None, None, 1, d), lambda i: (l, 1, 0, 0)),
        ],
        out_specs=rows(d),
        compiler_params=_cparams(("arbitrary",), est),
        name="out_stage",
    )(x, gate, o_att, y, g, bonus, lnx_g, lnx_b, w_o, ln_g, ln_b)


def _pick_tile(rows, target):
    t = min(rows, target)
    while rows % t:
        t //= 2
    return t


def kernel(x_prompt, x_sample, cache_k, cache_v, cache_logf, state_wkv, state_shift, page_table, c_prompt, c_sample, w_ada, b_ada, w_in, b_f, mu_shift, w_decay0, w_decay_lora, w_iclr0, w_iclr_lora, w_gate_lora, k_k, k_a, r_k, lnx_g, lnx_b, w_o, w_ffn_up, w_ffn_down, ln_g, ln_b):
    nb, s, d = x_prompt.shape
    nd, ds_, _ = x_sample.shape
    assert ds_ == 1, "the sample group decodes one token per sequence"
    depth = w_ada.shape[0]
    nh = b_f.shape[1]
    aw = nh * HEAD_DIM
    rw = w_decay0.shape[1]
    nhr = rw // HEAD_DIM
    p_att = 3 * aw + nh
    pw = w_in.shape[2] - p_att
    alpha = (2 * depth) ** 0.25

    w_rw = w_in[:, :, p_att:]
    w_ft = jnp.swapaxes(w_in[:, :, 3 * aw:p_att], 1, 2)
    dl, il = w_decay_lora.shape[1], w_iclr_lora.shape[1]
    assert dl == il and (dl + il) % LANES == 0
    wwa = jnp.concatenate([
        jnp.concatenate([w_decay_lora, jnp.zeros((depth, dl, rw), F32)], axis=2),
        jnp.concatenate([jnp.zeros((depth, il, rw), F32), w_iclr_lora], axis=2)], axis=1)
    vec3 = lambda a: a.reshape(depth, 1, -1)
    mu3, wd03, wi03, kk3, ka3, rk3, lg3, lb3 = map(vec3, (mu_shift, w_decay0, w_iclr0, k_k, k_a, r_k, lnx_g, lnx_b))
    ln_g4 = ln_g.reshape(depth, 3, 1, d)
    ln_b4 = ln_b.reshape(depth, 3, 1, d)

    mod = _ada(jnp.concatenate([c_prompt, c_sample], axis=0), w_ada, b_ada)
    mod = mod.reshape(depth, nb + nd, 9, d)

    def run_group(x, n_seq, t, mods, is_prompt, l, shift_prev, wkv0):
        rows = n_seq * t
        tm = _pick_tile(t, 1024) if is_prompt else rows
        rpm = t if is_prompt else rows
        seq_tiles = t // tm if is_prompt else 1
        shift, scale, gate = mods
        x = _ffn(x, shift[0], scale[0], gate[0], w_ffn_up, w_ffn_down, ln_g4, ln_b4, l, 0,
                 tm=tm, rows_per_mod=rpm, alpha=alpha)
        qkv, lf, lft = _proj(x, shift[1], scale[1], w_in, l, 3 * aw, tm=tm, rows_per_mod=rpm,
                             logf=(w_ft, b_f))
        p_rw = _proj(x, shift[1], scale[1], w_rw, l, pw, tm=tm, rows_per_mod=rpm)
        if is_prompt:
            c, ct = _fcum(lf, lft, n_seq, t)
            o_att = _fox_prompt(qkv, c, ct, n_seq, t)
            prev = p_rw
        else:
            o_att = _fox_decode(qkv, lf, page_table, cache_k, cache_v, cache_logf, l)
            prev = shift_prev
        tmp = _pick_tile(tm, 512)
        r_, w_, k_, v_, kk_, b_, g_, bonus = _rwkv_prep(
            p_rw, prev, mu3, wd03, wi03, wwa, w_gate_lora, kk3, ka3, rk3, l,
            tm=tmp, seq_tiles=t // tmp if is_prompt else 1, per_row_prev=not is_prompt)
        if is_prompt:
            y, wkv_new = _wkv(r_, w_, k_, v_, kk_, b_, None, n_seq, t,
                              nbc=_pick_tile(n_seq, 4), tb=_pick_tile(t, 256))
        else:
            y, wkv_new = _wkv(r_, w_, k_, v_, kk_, b_, wkv0, n_seq, t, nbc=_pick_tile(n_seq, 8), tb=1)
        x = _out_stage(x, gate[1], o_att, y, g_, bonus, lg3, lb3, w_o, ln_g4, ln_b4, l,
                       tm=_pick_tile(tm, 512), rows_per_mod=rpm, alpha=alpha)
        x = _ffn(x, shift[2], scale[2], gate[2], w_ffn_up, w_ffn_down, ln_g4, ln_b4, l, 1,
                 tm=tm, rows_per_mod=rpm, alpha=alpha)
        k_rows = qkv[:, aw:2 * aw].reshape(n_seq, t, nh, HEAD_DIM)
        v_rows = qkv[:, 2 * aw:3 * aw].reshape(n_seq, t, nh, HEAD_DIM)
        f_rows = lf.reshape(n_seq, t, nh)
        shift_new = p_rw.reshape(n_seq, t, pw)[:, -1]
        return x, k_rows, v_rows, f_rows, wkv_new, shift_new

    def group_mods(l, lo, hi, per_row):
        m = mod[l, lo:hi]
        get = lambda i: (m[:, i][None] if per_row else m[:, i][:, None])
        return ([get(3 * i) for i in range(3)], [get(3 * i + 1) for i in range(3)],
                [get(3 * i + 2) for i in range(3)])

    xp = x_prompt.reshape(nb * s, d)
    xs_ = x_sample.reshape(nd, d)
    outs_p, outs_s = [], []
    for l in range(depth):
        xp, *rest = run_group(xp, nb, s, group_mods(l, 0, nb, False), True, l, None, None)
        outs_p.append(rest)
        xs_, *rest = run_group(xs_, nd, 1, group_mods(l, nb, nb + nd, True), False, l,
                               state_shift[l], state_wkv[l])
        outs_s.append(rest)
    stack = lambda outs, i: jnp.stack([o[i] for o in outs])
    return (xp.reshape(nb, s, d), xs_.reshape(nd, 1, d),
            stack(outs_p, 0), stack(outs_p, 1), stack(outs_p, 2), stack(outs_p, 3), stack(outs_p, 4),
            stack(outs_s, 0), stack(outs_s, 1), stack(outs_s, 2), stack(outs_s, 3), stack(outs_s, 4))
```

```python
import functools

import jax
import jax.numpy as jnp
from jax import lax
from jax.experimental import pallas as pl
from jax.experimental.pallas import tpu as pltpu

F32 = jnp.float32
BF16 = jnp.bfloat16

HEAD_DIM = 64
LANES = 128
LN_EPS = 1e-5
GN_EPS = 64e-5
NEG_BIG = -1e30
VMEM_CAP = 60 * 1024 * 1024
WKV_NBC = 8
WKV_TB = 128


def _cparams(sem, est_bytes):
    limit = int(min(VMEM_CAP, max(32 * 1024 * 1024, est_bytes)))
    return pltpu.CompilerParams(dimension_semantics=sem, vmem_limit_bytes=limit)


def _dot(a, b):
    return jnp.dot(a, b, preferred_element_type=F32)


def _dot_nt(a, b):
    return lax.dot_general(a, b, (((1,), (1,)), ((), ())), preferred_element_type=F32)


def _split2(x):
    hi = x.astype(BF16)
    lo = (x - hi.astype(F32)).astype(BF16)
    return hi, lo


def _split3(x):
    hi = x.astype(BF16)
    r1 = x - hi.astype(F32)
    mid = r1.astype(BF16)
    lo = (r1 - mid.astype(F32)).astype(BF16)
    return hi, mid, lo


def _dot_exact_rhs(parts, m):
    acc = _dot(parts[0], m)
    for p in parts[1:]:
        acc = acc + _dot(p, m)
    return acc


def _sigmoid(x):
    return 1.0 / (1.0 + jnp.exp(-x))


def _softplus(x):
    return jnp.maximum(x, 0.0) + jnp.log1p(jnp.exp(-jnp.abs(x)))


def _layer_norm(y, g, b):
    mu = jnp.mean(y, axis=-1, keepdims=True)
    d = y - mu
    var = jnp.mean(d * d, axis=-1, keepdims=True)
    return d * lax.rsqrt(var + LN_EPS) * g + b


def _seg_ones(n):
    r = lax.broadcasted_iota(jnp.int32, (n, n), 0) // HEAD_DIM
    c = lax.broadcasted_iota(jnp.int32, (n, n), 1) // HEAD_DIM
    return (r == c).astype(BF16)


def _ada_kernel(c_ref, w_ref, b_ref, o_ref):
    c = c_ref[...]
    ca = (c * _sigmoid(c)).astype(BF16)
    o_ref[...] = _dot(ca, w_ref[...].astype(BF16)) + b_ref[...]


def _ada(c_all, w_ada, b_ada):
    depth, d, n = w_ada.shape
    m = c_all.shape[0]
    tn = n // 4 if (n // 4) % LANES == 0 and n % 4 == 0 else n
    est = 2 * (d * tn * 4 + m * tn * 4 + m * d * 4) + d * tn * 2 + (4 << 20)
    return pl.pallas_call(
        _ada_kernel,
        out_shape=jax.ShapeDtypeStruct((depth, m, n), F32),
        grid=(depth, n // tn),
        in_specs=[
            pl.BlockSpec((m, d), lambda l, j: (0, 0)),
            pl.BlockSpec((None, d, tn), lambda l, j: (l, 0, j)),
            pl.BlockSpec((None, 1, tn), lambda l, j: (l, 0, j)),
        ],
        out_specs=pl.BlockSpec((None, m, tn), lambda l, j: (l, 0, j)),
        compiler_params=_cparams(("arbitrary", "arbitrary"), est),
        name="ada_mod",
    )(c_all, w_ada, b_ada.reshape(depth, 1, n))


def _ffn_kernel(x_ref, sh_ref, sc_ref, gt_ref, wg_ref, wu_ref, wd_ref, g_ref, b_ref, o_ref,
                h_scr, acc_scr, *, alpha):
    j = pl.program_id(1)

    @pl.when(j == 0)
    def _():
        h_scr[...] = (x_ref[...] * (1.0 + sc_ref[0]) + sh_ref[0]).astype(BF16)
        acc_scr[...] = jnp.zeros_like(acc_scr)

    h = h_scr[...]
    gate = _dot(h, wg_ref[...].astype(BF16))
    up = _dot(h, wu_ref[...].astype(BF16))
    act = (gate * _sigmoid(gate) * up).astype(BF16)
    acc_scr[...] += _dot(act, wd_ref[...].astype(BF16))

    @pl.when(j == pl.num_programs(1) - 1)
    def _():
        y = alpha * x_ref[...] + 0.5 * (1.0 + gt_ref[0]) * acc_scr[...]
        o_ref[...] = _layer_norm(y, g_ref[...], b_ref[...])


def _mod_spec(mod, rows_per_mod, tm):
    blk = (1,) + mod.shape[1:]
    if mod.shape[0] == 1:
        return pl.BlockSpec(blk, lambda i, j: (0, 0, 0))
    tiles = rows_per_mod // tm
    return pl.BlockSpec(blk, lambda i, j: (i // tiles, 0, 0))


def _ffn(x, shift, scale, gate, w_up, w_down, ln_g, ln_b, l, s, *, tm, rows_per_mod, alpha):
    r, d = x.shape
    f = w_down.shape[2]
    tf = 256 if f % 256 == 0 else f
    nf = f // tf
    est = (2 * (2 * tm * d * 4 + 3 * d * tf * 4) + tm * d * 6 + 6 * tm * tf * 4 + 3 * d * tf * 2
           + (4 << 20))
    kern = functools.partial(_ffn_kernel, alpha=alpha)
    return pl.pallas_call(
        kern,
        out_shape=jax.ShapeDtypeStruct((r, d), F32),
        grid=(r // tm, nf),
        in_specs=[
            pl.BlockSpec((tm, d), lambda i, j: (i, 0)),
            _mod_spec(shift, rows_per_mod, tm),
            _mod_spec(scale, rows_per_mod, tm),
            _mod_spec(gate, rows_per_mod, tm),
            pl.BlockSpec((None, None, d, tf), lambda i, j: (l, s, 0, j)),
            pl.BlockSpec((None, None, d, tf), lambda i, j: (l, s, 0, j + nf)),
            pl.BlockSpec((None, None, tf, d), lambda i, j: (l, s, j, 0)),
            pl.BlockSpec((None, None, 1, d), lambda i, j: (l, 2 * s, 0, 0)),
            pl.BlockSpec((None, None, 1, d), lambda i, j: (l, 2 * s, 0, 0)),
        ],
        out_specs=pl.BlockSpec((tm, d), lambda i, j: (i, 0)),
        scratch_shapes=[pltpu.VMEM((tm, d), BF16), pltpu.VMEM((tm, d), F32)],
        compiler_params=_cparams(("arbitrary", "arbitrary"), est),
        name="ffn",
    )(x, shift, scale, gate, w_up, w_up, w_down, ln_g, ln_b)


def _proj_kernel(x_ref, sh_ref, sc_ref, w_ref, o_ref, h_scr):
    @pl.when(pl.program_id(1) == 0)
    def _():
        h_scr[...] = (x_ref[...] * (1.0 + sc_ref[0]) + sh_ref[0]).astype(BF16)

    o_ref[...] = _dot(h_scr[...], w_ref[...].astype(BF16))


def _proj_logf_kernel(x_ref, sh_ref, sc_ref, w_ref, wf_ref, bfr_ref, bfc_ref, o_ref, lf_ref, lft_ref,
                      h_scr):
    @pl.when(pl.program_id(1) == 0)
    def _():
        h = (x_ref[...] * (1.0 + sc_ref[0]) + sh_ref[0]).astype(BF16)
        h_scr[...] = h
        wf = wf_ref[...].astype(BF16)
        z = _dot_nt(h, wf) + bfr_ref[...]
        lf_ref[...] = -_softplus(-z)
        zt = _dot_nt(wf, h) + bfc_ref[...]
        lft_ref[...] = -_softplus(-zt)

    o_ref[...] = _dot(h_scr[...], w_ref[...].astype(BF16))


def _proj(x, shift, scale, w, l, n, *, tm, rows_per_mod, logf=None):
    r, d = x.shape
    tn = n // 2 if (n // 2) % LANES == 0 else n
    est = 2 * (tm * d * 4 + d * tn * 4 + tm * tn * 4) + tm * d * 2 + d * tn * 2 + tm * tn * 4 + (4 << 20)
    in_specs = [
        pl.BlockSpec((tm, d), lambda i, j: (i, 0)),
        _mod_spec(shift, rows_per_mod, tm),
        _mod_spec(scale, rows_per_mod, tm),
        pl.BlockSpec((None, d, tn), lambda i, j: (l, 0, j)),
    ]
    o_spec = pl.BlockSpec((tm, tn), lambda i, j: (i, j))
    o_shape = jax.ShapeDtypeStruct((r, n), F32)
    if logf is None:
        return pl.pallas_call(
            _proj_kernel, out_shape=o_shape, grid=(r // tm, n // tn), in_specs=in_specs,
            out_specs=o_spec, scratch_shapes=[pltpu.VMEM((tm, d), BF16)],
            compiler_params=_cparams(("arbitrary", "arbitrary"), est), name="proj",
        )(x, shift, scale, w)
    w_ft, b_f = logf
    h = w_ft.shape[1]
    in_specs += [
        pl.BlockSpec((None, h, d), lambda i, j: (l, 0, 0)),
        pl.BlockSpec((None, 1, h), lambda i, j: (l, 0, 0)),
        pl.BlockSpec((None, h, 1), lambda i, j: (l, 0, 0)),
    ]
    return pl.pallas_call(
        _proj_logf_kernel,
        out_shape=(o_shape, jax.ShapeDtypeStruct((r, h), F32), jax.ShapeDtypeStruct((h, r), F32)),
        grid=(r // tm, n // tn), in_specs=in_specs,
        out_specs=(o_spec, pl.BlockSpec((tm, h), lambda i, j: (i, 0)),
                   pl.BlockSpec((h, tm), lambda i, j: (0, i))),
        scratch_shapes=[pltpu.VMEM((tm, d), BF16)],
        compiler_params=_cparams(("arbitrary", "arbitrary"), est), name="proj_logf",
    )(x, shift, scale, w, w_ft, b_f.reshape(-1, 1, h), b_f.reshape(-1, h, 1))


def _fcum_kernel(lf_ref, lft_ref, c_ref, ct_ref, *, cb):
    s, h = lf_ref.shape
    r = lax.broadcasted_iota(jnp.int32, (cb, cb), 0)
    c = lax.broadcasted_iota(jnp.int32, (cb, cb), 1)
    low = (c <= r).astype(BF16)
    upp = (r <= c).astype(BF16)
    carry = jnp.zeros((1, h), F32)
    carry_t = jnp.zeros((h, 1), F32)
    for blk in range(s // cb):
        x = lf_ref[blk * cb:(blk + 1) * cb, :]
        xs = _split3(x)
        cs = _dot(low, xs[0]) + _dot(low, xs[1]) + _dot(low, xs[2]) + carry
        c_ref[blk * cb:(blk + 1) * cb, :] = cs
        carry = cs[cb - 1:cb, :]
        xt = lft_ref[:, blk * cb:(blk + 1) * cb]
        cst = _dot_exact_rhs(_split3(xt), upp) + carry_t
        ct_ref[:, blk * cb:(blk + 1) * cb] = cst
        carry_t = cst[:, cb - 1:cb]


def _fcum(lf, lft, nb, s):
    h = lf.shape[1]
    cb = min(256, s)
    return pl.pallas_call(
        functools.partial(_fcum_kernel, cb=cb),
        out_shape=(jax.ShapeDtypeStruct(lf.shape, F32), jax.ShapeDtypeStruct(lft.shape, F32)),
        grid=(nb,),
        in_specs=[pl.BlockSpec((s, h), lambda b: (b, 0)), pl.BlockSpec((h, s), lambda b: (0, b))],
        out_specs=(pl.BlockSpec((s, h), lambda b: (b, 0)), pl.BlockSpec((h, s), lambda b: (0, b))),
        compiler_params=_cparams(("arbitrary",), 0),
        name="fcum",
    )(lf, lft)


def _fox_prompt_kernel(q_ref, k_ref, v_ref, c_ref, ct_ref, o_ref, kb_scr, vb_scr, *, tq):
    hp = pl.program_id(1)
    i = pl.program_id(2)

    @pl.when(i == 0)
    def _():
        kb_scr[...] = k_ref[...].astype(BF16)
        vb_scr[...] = v_ref[...].astype(BF16)

    q = q_ref[...] * (HEAD_DIM ** -0.5)
    lane = lax.broadcasted_iota(jnp.int32, (1, LANES), 1)
    hlane = lax.broadcasted_iota(jnp.int32, (1, c_ref.shape[1]), 1)
    row = lax.broadcasted_iota(jnp.int32, (tq, tq), 0)
    col = lax.broadcasted_iota(jnp.int32, (tq, tq), 1)
    in_half = [(lane // HEAD_DIM) == hh for hh in range(2)]
    qm = [jnp.where(in_half[hh], q, 0.0).astype(BF16) for hh in range(2)]
    c_i = [jnp.sum(jnp.where(hlane == 2 * hp + hh, c_ref[...], 0.0), axis=-1, keepdims=True)
           for hh in range(2)]

    def body(kj, carry):
        off = pl.multiple_of(kj * tq, tq)
        kblk = kb_scr[pl.ds(off, tq), :]
        vblk = vb_scr[pl.ds(off, tq), :]
        visible = col <= row + (i - kj) * tq
        new = []
        for hh in range(2):
            m, l, acc = carry[hh]
            s = _dot_nt(qm[hh], kblk) + c_i[hh] - ct_ref[pl.ds(2 * hp + hh, 1), pl.ds(off, tq)]
            s = jnp.where(visible, s, NEG_BIG)
            m_new = jnp.maximum(m, jnp.max(s, axis=-1, keepdims=True))
            p = jnp.exp(s - m_new)
            a = jnp.exp(m - m_new)
            l = a * l + jnp.sum(p, axis=-1, keepdims=True)
            acc = a * acc + _dot(p.astype(BF16), vblk)
            new.append((m_new, l, acc))
        return tuple(new)

    init = (jnp.full((tq, 1), NEG_BIG, F32), jnp.zeros((tq, 1), F32), jnp.zeros((tq, LANES), F32))
    (_, l0, acc0), (_, l1, acc1) = lax.fori_loop(0, i + 1, body, (init, init))
    o_ref[...] = jnp.where(in_half[0], acc0 / l0, acc1 / l1)


def _fox_prompt(qkv, c, ct, nb, s):
    r = qkv.shape[0]
    aw = qkv.shape[1] // 3
    nhp = aw // LANES
    tq = min(256, s)
    nq = s // tq
    h = c.shape[1]
    est = 2 * (2 * s * LANES * 4 + 2 * tq * LANES * 4 + tq * LANES * 4 + 8 * s * 4) + 2 * s * LANES * 2 \
        + 8 * tq * tq * 4 + (4 << 20)
    return pl.pallas_call(
        functools.partial(_fox_prompt_kernel, tq=tq),
        out_shape=jax.ShapeDtypeStruct((r, aw), F32),
        grid=(nb, nhp, nq),
        in_specs=[
            pl.BlockSpec((tq, LANES), lambda b, p, i: (b * nq + i, p)),
            pl.BlockSpec((s, LANES), lambda b, p, i: (b, nhp + p)),
            pl.BlockSpec((s, LANES), lambda b, p, i: (b, 2 * nhp + p)),
            pl.BlockSpec((tq, h), lambda b, p, i: (b * nq + i, 0)),
            pl.BlockSpec((h, s), lambda b, p, i: (0, b)),
        ],
        out_specs=pl.BlockSpec((tq, LANES), lambda b, p, i: (b * nq + i, p)),
        scratch_shapes=[pltpu.VMEM((s, LANES), BF16), pltpu.VMEM((s, LANES), BF16)],
        compiler_params=_cparams(("arbitrary", "arbitrary", "arbitrary"), est),
        name="fox_prompt",
    )(qkv, qkv, qkv, c, ct)


def _fox_decode_kernel(pt_ref, q_ref, kn_ref, vn_ref, lf_ref, *rest, pps):
    kc_refs, vc_refs, fc_refs = rest[0:pps], rest[pps:2 * pps], rest[2 * pps:3 * pps]
    o_ref, m_scr, l_scr, acc_scr, tot_scr = rest[3 * pps:]
    j = pl.program_id(1)
    nh, dh, ps = kc_refs[0].shape

    @pl.when(j == 0)
    def _():
        m_scr[...] = jnp.full_like(m_scr, NEG_BIG)
        l_scr[...] = jnp.zeros_like(l_scr)
        acc_scr[...] = jnp.zeros_like(acc_scr)
        tot_scr[...] = jnp.zeros_like(tot_scr)

    q = q_ref[0] * (HEAD_DIM ** -0.5)
    qb = q.astype(BF16)
    r = lax.broadcasted_iota(jnp.int32, (ps, ps), 0)
    c = lax.broadcasted_iota(jnp.int32, (ps, ps), 1)
    later = (r > c).astype(BF16)
    rowid = lax.broadcasted_iota(jnp.int32, (nh, 1), 0)
    tot = tot_scr[...]
    parts = []
    for u in range(pps):
        lpt = fc_refs[u][...]
        suffix = _dot_exact_rhs(_split3(lpt), later) + tot
        tot = tot + jnp.sum(lpt, axis=-1, keepdims=True)
        lg = jnp.zeros((nh, ps), F32)
        for h in range(nh):
            lg = jnp.where(rowid == h, _dot(qb, kc_refs[u][h].astype(BF16)), lg)
        parts.append(lg + suffix)
    tot_scr[...] = tot
    logits = jnp.concatenate(parts, axis=-1) + lf_ref[0]
    m_old = m_scr[...]
    m_new = jnp.maximum(m_old, jnp.max(logits, axis=-1, keepdims=True))
    p = jnp.exp(logits - m_new)
    a = jnp.exp(m_old - m_new)
    l_scr[...] = a * l_scr[...] + jnp.sum(p, axis=-1, keepdims=True)
    pb = p.astype(BF16)
    pv = jnp.zeros((nh, dh), F32)
    for h in range(nh):
        res = _dot_nt(pb[:, 0:ps], vc_refs[0][h].astype(BF16))
        for u in range(1, pps):
            res = res + _dot_nt(pb[:, u * ps:(u + 1) * ps], vc_refs[u][h].astype(BF16))
        pv = jnp.where(rowid == h, res, pv)
    acc_scr[...] = a * acc_scr[...] + pv
    m_scr[...] = m_new

    @pl.when(j == pl.num_programs(1) - 1)
    def _():
        s_new = jnp.sum(q * kn_ref[0], axis=-1, keepdims=True)
        m_all = jnp.maximum(m_scr[...], s_new)
        a2 = jnp.exp(m_scr[...] - m_all)
        p_new = jnp.exp(s_new - m_all)
        o_ref[0] = (a2 * acc_scr[...] + p_new * vn_ref[0]) / (a2 * l_scr[...] + p_new)


def _fox_decode(qkv, lf, page_table, cache_k, cache_v, cache_logf, l):
    n, n_pages = page_table.shape
    _, n_pool, ps, nh, dh = cache_k.shape
    aw = nh * dh
    kc = jnp.transpose(cache_k, (0, 1, 3, 4, 2))
    vc = jnp.transpose(cache_v, (0, 1, 3, 4, 2))
    fc = jnp.transpose(cache_logf, (0, 1, 3, 2))
    pt = page_table.reshape(-1)

    pps = 4 if n_pages % 4 == 0 else (2 if n_pages % 2 == 0 else 1)

    def tok(i, j, pt_ref):
        return (i, 0, 0)

    def cache_spec(u, *tail):
        def index(i, j, pt_ref):
            return (l, pt_ref[i * n_pages + (n_pages - 1 - (j * pps + u))]) + (0,) * len(tail)
        return pl.BlockSpec((None, None) + tail, index)

    grid_spec = pltpu.PrefetchScalarGridSpec(
        num_scalar_prefetch=1,
        grid=(n, n_pages // pps),
        in_specs=[
            pl.BlockSpec((1, nh, dh), tok), pl.BlockSpec((1, nh, dh), tok), pl.BlockSpec((1, nh, dh), tok),
            pl.BlockSpec((1, nh, 1), tok),
        ] + [cache_spec(u, nh, dh, ps) for u in range(pps)] * 2 + [cache_spec(u, nh, ps) for u in range(pps)],
        out_specs=pl.BlockSpec((1, nh, dh), tok),
        scratch_shapes=[pltpu.VMEM((nh, 1), F32), pltpu.VMEM((nh, 1), F32),
                        pltpu.VMEM((nh, dh), F32), pltpu.VMEM((nh, 1), F32)],
    )
    heads = lambda i: qkv[:, i * aw:(i + 1) * aw].reshape(n, nh, dh)
    out = pl.pallas_call(
        functools.partial(_fox_decode_kernel, pps=pps),
        out_shape=jax.ShapeDtypeStruct((n, nh, dh), F32),
        grid_spec=grid_spec,
        compiler_params=_cparams(("arbitrary", "arbitrary"), 0),
        name="fox_decode",
    )(pt, heads(0), heads(1), heads(2), lf.reshape(n, nh, 1), *([kc] * pps + [vc] * pps + [fc] * pps))
    return out.reshape(n, aw)


def _rwkv_prep_kernel(p_ref, prev_ref, mu_ref, wd0_ref, wi0_ref, wwa_ref, wg_ref, kk_ref, ka_ref, rk_ref,
                      r_o, w_o, k_o, v_o, kk_o, b_o, g_o, bonus_o, *, rw, seq_tiles, per_row_prev):
    p = p_ref[...]
    tm = p.shape[0]
    if per_row_prev:
        p_prev = prev_ref[...]
    else:
        first = (pl.program_id(0) % seq_tiles) == 0
        prev8 = prev_ref[...]
        before = jnp.where(first, 0.0, prev8[7:8, :])
        rolled = pltpu.roll(p, 1, 0)
        rowi = lax.broadcasted_iota(jnp.int32, (tm, 1), 0)
        p_prev = jnp.where(rowi == 0, before, rolled)
    xs = p + mu_ref[...] * (p_prev - p)
    r = xs[:, 0:rw]
    k = xs[:, rw:2 * rw]
    v = xs[:, 2 * rw:3 * rw]
    nwa = wwa_ref.shape[0]
    dl = nwa // 2
    wa = xs[:, 3 * rw:3 * rw + nwa]
    lane = lax.broadcasted_iota(jnp.int32, (1, nwa), 1)
    wa_t = jnp.where(lane < dl, jnp.tanh(wa), wa).astype(BF16)
    lora = _dot(wa_t, wwa_ref[...].astype(BF16))
    w = -_softplus(-(wd0_ref[...] + lora[:, 0:rw])) - 0.5
    decay = jnp.exp(-jnp.exp(w))
    a = _sigmoid(wi0_ref[...] + lora[:, rw:2 * rw])
    xg = xs[:, 3 * rw + nwa:]
    g = _dot(_sigmoid(xg).astype(BF16), wg_ref[...].astype(BF16))
    seg = _seg_ones(rw)
    kk = k * kk_ref[...]
    ss = _dot_exact_rhs(_split2(kk * kk), seg)
    kk = kk / jnp.maximum(jnp.sqrt(ss), 1e-12)
    k2 = k * (1.0 + (a - 1.0) * ka_ref[...])
    bonus = _dot_exact_rhs(_split2(r * k2 * rk_ref[...]), seg) * v
    r_o[...] = r
    w_o[...] = decay
    k_o[...] = k2
    v_o[...] = v
    kk_o[...] = kk
    b_o[...] = kk * a
    g_o[...] = g
    bonus_o[...] = bonus


def _rwkv_prep(p, prev, mu, wd0, wi0, wwa, wg, k_k, k_a, r_k, l, *, tm, seq_tiles, per_row_prev):
    r, pw = p.shape
    rw = wd0.shape[-1]
    nwa = wwa.shape[1]
    ng = wg.shape[1]
    if per_row_prev:
        prev_spec = pl.BlockSpec((tm, pw), lambda i: (i, 0))
    else:
        prev_spec = pl.BlockSpec((8, pw), lambda i: (jnp.maximum(i * (tm // 8) - 1, 0), 0))

    def vec(n):
        return pl.BlockSpec((None, 1, n), lambda i: (l, 0, 0))

    o_spec = pl.BlockSpec((tm, rw), lambda i: (i, 0))
    o_shape = jax.ShapeDtypeStruct((r, rw), F32)
    est = 2 * (tm * pw * 4 + 8 * tm * rw * 4) + 12 * tm * rw * 4 + 3 * tm * pw * 4 + (6 << 20)
    return pl.pallas_call(
        functools.partial(_rwkv_prep_kernel, rw=rw, seq_tiles=seq_tiles, per_row_prev=per_row_prev),
        out_shape=(o_shape,) * 8,
        grid=(r // tm,),
        in_specs=[
            pl.BlockSpec((tm, pw), lambda i: (i, 0)), prev_spec, vec(pw), vec(rw), vec(rw),
            pl.BlockSpec((None, nwa, 2 * rw), lambda i: (l, 0, 0)),
            pl.BlockSpec((None, ng, rw), lambda i: (l, 0, 0)),
            vec(rw), vec(rw), vec(rw),
        ],
        out_specs=(o_spec,) * 8,
        compiler_params=_cparams(("arbitrary",), est),
        name="rwkv_prep",
    )(p, prev, mu, wd0, wi0, wwa, wg, k_k, k_a, r_k)


def _wkv_kernel(*refs, nbc, npair, tb, zero_init):
    if zero_init:
        r_ref, w_ref, k_ref, v_ref, kk_ref, b_ref, y_ref, so_ref, s_scr = refs
        s0_ref = None
    else:
        r_ref, w_ref, k_ref, v_ref, kk_ref, b_ref, s0_ref, y_ref, so_ref, s_scr = refs
    tj = pl.program_id(1)
    hd = HEAD_DIM

    @pl.when(tj == 0)
    def _():
        if zero_init:
            s_scr[...] = jnp.zeros_like(s_scr)
        else:
            for bi in range(nbc):
                for pp in range(npair):
                    s_scr[bi * npair + pp] = jnp.concatenate(
                        [s0_ref[bi, 2 * pp], s0_ref[bi, 2 * pp + 1]], axis=-1)

    seg = _seg_ones(LANES)
    eye2 = (lax.broadcasted_iota(jnp.int32, (hd, LANES), 0)
            == lax.broadcasted_iota(jnp.int32, (hd, LANES), 1) % hd).astype(F32)

    grp = 8 if tb % 8 == 0 else tb
    chains = [(bi, pp) for bi in range(nbc) for pp in range(npair)]

    nc = len(chains)
    eye3 = eye2[None]

    def seg_sum(x3, parts):
        x2 = x3.reshape(nc * hd, LANES)
        ops = _split2(x2) if parts == 2 else (x2.astype(BF16),)
        return _dot_exact_rhs(ops, seg).reshape(nc, hd, LANES)

    def step_group(tg, carry):
        t0 = 0 if tb == grp else pl.multiple_of(tg * grp, grp)
        blocks = [[ref[bi, pl.ds(t0, grp), pl.ds(pp * LANES, LANES)] for bi, pp in chains]
                  for ref in (r_ref, w_ref, k_ref, v_ref, kk_ref, b_ref)]

        def rows_at(vec, j):
            return jnp.concatenate([blk[j:j + 1, :][None] for blk in blocks[vec]], axis=0)

        s = s_scr[...]
        ys = []
        for j in range(grp):
            r3, w3, k3, v3, kk3, b3 = [rows_at(i, j) for i in range(6)]
            vb = seg_sum(eye3 * v3, 2)
            sa = seg_sum(s * (-kk3), 2)
            s = s * w3 + sa * b3 + vb * k3
            yb = seg_sum(s * r3, 1)
            ys.append(jnp.sum(yb * eye3, axis=1, keepdims=True))
        s_scr[...] = s
        for ci, (bi, pp) in enumerate(chains):
            y_ref[bi, pl.ds(t0, grp), pl.ds(pp * LANES, LANES)] = jnp.concatenate(
                [ys[j][ci] for j in range(grp)], axis=0)
        return carry

    if tb == grp:
        step_group(0, 0)
    else:
        lax.fori_loop(0, tb // grp, step_group, 0)

    @pl.when(tj == pl.num_programs(1) - 1)
    def _():
        for bi in range(nbc):
            for pp in range(npair):
                s = s_scr[bi * npair + pp]
                so_ref[bi, 2 * pp] = s[:, 0:hd]
                so_ref[bi, 2 * pp + 1] = s[:, hd:2 * hd]


def _wkv(r, w, k, v, kk, b, state0, nb, t, *, nbc, tb):
    rw = r.shape[1]
    nh = rw // HEAD_DIM
    npair = nh // 2
    ins = [x.reshape(nb, t, rw) for x in (r, w, k, v, kk, b)]
    seq_spec = pl.BlockSpec((nbc, tb, rw), lambda g, j: (g, j, 0))
    st_spec = pl.BlockSpec((nbc, nh, HEAD_DIM, HEAD_DIM), lambda g, j: (g, 0, 0, 0))
    in_specs = [seq_spec] * 6
    if state0 is not None:
        ins.append(state0)
        in_specs.append(st_spec)
    est = 2 * 7 * nbc * tb * rw * 4 + 6 * nbc * nh * HEAD_DIM * LANES * 4 + (6 << 20)
    y, s_out = pl.pallas_call(
        functools.partial(_wkv_kernel, nbc=nbc, npair=npair, tb=tb, zero_init=state0 is None),
        out_shape=(jax.ShapeDtypeStruct((nb, t, rw), F32),
                   jax.ShapeDtypeStruct((nb, nh, HEAD_DIM, HEAD_DIM), F32)),
        grid=(nb // nbc, t // tb),
        in_specs=in_specs,
        out_specs=(seq_spec, st_spec),
        scratch_shapes=[pltpu.VMEM((nbc * npair, HEAD_DIM, LANES), F32)],
        compiler_params=_cparams(("arbitrary", "arbitrary"), est),
        name="wkv_scan",
    )(*ins)
    return y.reshape(nb * t, rw), s_out


def _out_kernel(x_ref, gt_ref, oa_ref, y_ref, g_ref, bonus_ref, lg_ref, lb_ref, wo_ref, ng_ref, nb_ref,
                o_ref, *, alpha, aw):
    y = y_ref[...]
    rw = y.shape[1]
    seg = _seg_ones(rw)
    inv = 1.0 / HEAD_DIM
    m = _dot_exact_rhs(_split2(y), seg) * inv
    d = y - m
    var = _dot_exact_rhs(_split2(d * d), seg) * inv
    yn = d * lax.rsqrt(var + GN_EPS) * lg_ref[...] + lb_ref[...]
    o_rwkv = ((yn + bonus_ref[...]) * g_ref[...]).astype(BF16)
    mix = _dot(oa_ref[...].astype(BF16), wo_ref[0:aw, :].astype(BF16)) \
        + _dot(o_rwkv, wo_ref[aw:, :].astype(BF16))
    z = alpha * x_ref[...] + (1.0 + gt_ref[0]) * mix
    o_ref[...] = _layer_norm(z, ng_ref[...], nb_ref[...])


def _out_stage(x, gate, o_att, y, g, bonus, lnx_g, lnx_b, w_o, ln_g, ln_b, l, *, tm, rows_per_mod, alpha):
    r, d = x.shape
    aw = o_att.shape[1]
    rw = y.shape[1]
    blk = (1,) + gate.shape[1:]
    if gate.shape[0] == 1:
        gate_spec = pl.BlockSpec(blk, lambda i: (0, 0, 0))
    else:
        tiles = rows_per_mod // tm
        gate_spec = pl.BlockSpec(blk, lambda i: (i // tiles, 0, 0))

    def rows(n):
        return pl.BlockSpec((tm, n), lambda i: (i, 0))

    est = 2 * (2 * tm * d * 4 + tm * aw * 4 + 3 * tm * rw * 4 + d * d * 4) + d * d * 2 + 8 * tm * rw * 4 \
        + 3 * tm * d * 4 + (4 << 20)
    return pl.pallas_call(
        functools.partial(_out_kernel, alpha=alpha, aw=aw),
        out_shape=jax.ShapeDtypeStruct((r, d), F32),
        grid=(r // tm,),
        in_specs=[
            rows(d), gate_spec, rows(aw), rows(rw), rows(rw), rows(rw),
            pl.BlockSpec((None, 1, rw), lambda i: (l, 0, 0)),
            pl.BlockSpec((None, 1, rw), lambda i: (l, 0, 0)),
            pl.BlockSpec((None, d, d), lambda i: (l, 0, 0)),
            pl.BlockSpec((None, None, 1, d), lambda i: (l, 1, 0, 0)),
            pl.BlockSpec((None, None, 1, d), lambda i: (l, 1, 0, 0)),
        ],
        out_specs=rows(d),
        compiler_params=_cparams(("arbitrary",), est),
        name="out_stage",
    )(x, gate, o_att, y, g, bonus, lnx_g, lnx_b, w_o, ln_g, ln_b)


def _pick_tile(rows, target):
    t = min(rows, target)
    while rows % t:
        t //= 2
    return t


def kernel(x_prompt, x_sample, cache_k, cache_v, cache_logf, state_wkv, state_shift, page_table, c_prompt, c_sample, w_ada, b_ada, w_in, b_f, mu_shift, w_decay0, w_decay_lora, w_iclr0, w_iclr_lora, w_gate_lora, k_k, k_a, r_k, lnx_g, lnx_b, w_o, w_ffn_up, w_ffn_down, ln_g, ln_b):
    nb, s, d = x_prompt.shape
    nd, ds_, _ = x_sample.shape
    assert ds_ == 1, "the sample group decodes one token per sequence"
    depth = w_ada.shape[0]
    nh = b_f.shape[1]
    aw = nh * HEAD_DIM
    rw = w_decay0.shape[1]
    nhr = rw // HEAD_DIM
    p_att = 3 * aw + nh
    pw = w_in.shape[2] - p_att
    alpha = (2 * depth) ** 0.25

    w_rw = w_in[:, :, p_att:]
    w_ft = jnp.swapaxes(w_in[:, :, 3 * aw:p_att], 1, 2)
    dl, il = w_decay_lora.shape[1], w_iclr_lora.shape[1]
    assert dl == il and (dl + il) % LANES == 0
    wwa = jnp.concatenate([
        jnp.concatenate([w_decay_lora, jnp.zeros((depth, dl, rw), F32)], axis=2),
        jnp.concatenate([jnp.zeros((depth, il, rw), F32), w_iclr_lora], axis=2)], axis=1)
    vec3 = lambda a: a.reshape(depth, 1, -1)
    mu3, wd03, wi03, kk3, ka3, rk3, lg3, lb3 = map(vec3, (mu_shift, w_decay0, w_iclr0, k_k, k_a, r_k, lnx_g, lnx_b))
    ln_g4 = ln_g.reshape(depth, 3, 1, d)
    ln_b4 = ln_b.reshape(depth, 3, 1, d)

    mod = _ada(jnp.concatenate([c_prompt, c_sample], axis=0), w_ada, b_ada)
    mod = mod.reshape(depth, nb + nd, 9, d)

    def run_group(x, n_seq, t, mods, is_prompt, l, shift_prev, wkv0):
        rows = n_seq * t
        tm = _pick_tile(t, 1024) if is_prompt else rows
        rpm = t if is_prompt else rows
        seq_tiles = t // tm if is_prompt else 1
        shift, scale, gate = mods
        x = _ffn(x, shift[0], scale[0], gate[0], w_ffn_up, w_ffn_down, ln_g4, ln_b4, l, 0,
                 tm=tm, rows_per_mod=rpm, alpha=alpha)
        qkv, lf, lft = _proj(x, shift[1], scale[1], w_in, l, 3 * aw, tm=tm, rows_per_mod=rpm,
                             logf=(w_ft, b_f))
        p_rw = _proj(x, shift[1], scale[1], w_rw, l, pw, tm=tm, rows_per_mod=rpm)
        if is_prompt:
            c, ct = _fcum(lf, lft, n_seq, t)
            o_att = _fox_prompt(qkv, c, ct, n_seq, t)
            prev = p_rw
        else:
            o_att = _fox_decode(qkv, lf, page_table, cache_k, cache_v, cache_logf, l)
            prev = shift_prev
        tmp = _pick_tile(tm, 512)
        r_, w_, k_, v_, kk_, b_, g_, bonus = _rwkv_prep(
            p_rw, prev, mu3, wd03, wi03, wwa, w_gate_lora, kk3, ka3, rk3, l,
            tm=tmp, seq_tiles=t // tmp if is_prompt else 1, per_row_prev=not is_prompt)
        if is_prompt:
            y, wkv_new = _wkv(r_, w_, k_, v_, kk_, b_, None, n_seq, t,
                              nbc=_pick_tile(n_seq, WKV_NBC), tb=_pick_tile(t, WKV_TB))
        else:
            y, wkv_new = _wkv(r_, w_, k_, v_, kk_, b_, wkv0, n_seq, t, nbc=_pick_tile(n_seq, 8), tb=1)
        x = _out_stage(x, gate[1], o_att, y, g_, bonus, lg3, lb3, w_o, ln_g4, ln_b4, l,
                       tm=_pick_tile(tm, 512), rows_per_mod=rpm, alpha=alpha)
        x = _ffn(x, shift[2], scale[2], gate[2], w_ffn_up, w_ffn_down, ln_g4, ln_b4, l, 1,
                 tm=tm, rows_per_mod=rpm, alpha=alpha)
        k_rows = qkv[:, aw:2 * aw].reshape(n_seq, t, nh, HEAD_DIM)
        v_rows = qkv[:, 2 * aw:3 * aw].reshape(n_seq, t, nh, HEAD_DIM)
        f_rows = lf.reshape(n_seq, t, nh)
        shift_new = p_rw.reshape(n_seq, t, pw)[:, -1]
        return x, k_rows, v_rows, f_rows, wkv_new, shift_new

    def group_mods(l, lo, hi, per_row):
        m = mod[l, lo:hi]
        get = lambda i: (m[:, i][None] if per_row else m[:, i][:, None])
        return ([get(3 * i) for i in range(3)], [get(3 * i + 1) for i in range(3)],
                [get(3 * i + 2) for i in range(3)])

    xp = x_prompt.reshape(nb * s, d)
    xs_ = x_sample.reshape(nd, d)
    outs_p, outs_s = [], []
    for l in range(depth):
        xp, *rest = run_group(xp, nb, s, group_mods(l, 0, nb, False), True, l, None, None)
        outs_p.append(rest)
        xs_, *rest = run_group(xs_, nd, 1, group_mods(l, nb, nb + nd, True), False, l,
                               state_shift[l], state_wkv[l])
        outs_s.append(rest)
    stack = lambda outs, i: jnp.stack([o[i] for o in outs])
    return (xp.reshape(nb, s, d), xs_.reshape(nd, 1, d),
            stack(outs_p, 0), stack(outs_p, 1), stack(outs_p, 2), stack(outs_p, 3), stack(outs_p, 4),
            stack(outs_s, 0), stack(outs_s, 1), stack(outs_s, 2), stack(outs_s, 3), stack(outs_s, 4))
```
